```python
import math
import jax, jax.numpy as jnp
from jax import lax
import numpy as np

D_MODEL = 2048
BATCH = 4
SEQ = 2048
DEPTH = 2
DEC_BATCH = 128
DEC_SEQ = 4
PAST_LEN = 8192
PAGE_SIZE = 128

N_A_LAYERS = DEPTH // 2
N_B_LAYERS = DEPTH - N_A_LAYERS
HGRN_EXPAND = 128
HGRN_HEADS = D_MODEL // HGRN_EXPAND
HGRN_DK = HGRN_EXPAND
HGRN_DV = D_MODEL // HGRN_HEADS
HGRN_WIDTH = HGRN_HEADS * HGRN_DK
HGRN_CHUNK = 16
ATTN_HEAD_DIM = 64
ATTN_Q_HEADS = D_MODEL // ATTN_HEAD_DIM
ATTN_KV_HEADS = max(1, ATTN_Q_HEADS // 8)
ATTN_GROUP = ATTN_Q_HEADS // ATTN_KV_HEADS
WINDOW = 128
ROT_DIM = ATTN_HEAD_DIM // 4
ROPE_THETA = 500000.0
D_FF = int(math.ceil(8 * D_MODEL / 3 / 256)) * 256
RMS_EPS = 1e-6

kernel_name = "yoco_hgrn2_swa_sink_decode_step"


def rmsnorm(x, g):
    xf = x.astype(jnp.float32)
    y = xf * lax.rsqrt(jnp.mean(xf * xf, axis=-1, keepdims=True) + RMS_EPS)
    return (y * g.astype(jnp.float32)).astype(x.dtype)


def swiglu(xn, w_gate_up, w_down):
    gate, up = jnp.split(xn @ w_gate_up, 2, axis=-1)
    return (jax.nn.silu(gate) * up) @ w_down


def rope(x, pos):
    inv = ROPE_THETA ** (-jnp.arange(0, ROT_DIM, 2, dtype=jnp.float32) / ROT_DIM)
    ang = pos.astype(jnp.float32)[:, None] * inv[None, :]
    cos = jnp.cos(ang)[None, :, None, :]
    sin = jnp.sin(ang)[None, :, None, :]
    xr = x[..., :ROT_DIM].astype(jnp.float32)
    x1, x2 = jnp.split(xr, 2, axis=-1)
    rot = jnp.concatenate([x1 * cos - x2 * sin, x2 * cos + x1 * sin], axis=-1)
    return jnp.concatenate([rot.astype(x.dtype), x[..., ROT_DIM:]], axis=-1)


def gla_chunked(q, k, v, log_f, S0):
    B, T, H, DK = q.shape
    DV = v.shape[-1]
    C = math.gcd(T, HGRN_CHUNK)
    n = T // C

    def to_chunks(a):
        return jnp.moveaxis(a.astype(jnp.float32).reshape(B, n, C, *a.shape[2:]), 1, 0)

    tril = jnp.tril(jnp.ones((C, C), dtype=bool))

    def step(S, inp):
        qc, kc, vc, gc = inp
        b = jnp.cumsum(gc, axis=1)
        o_inter = jnp.einsum('bthk,bhkv->bthv', qc * jnp.exp(b), S)
        diff = b[:, :, None] - b[:, None, :]
        decay = jnp.exp(jnp.where(tril[None, :, :, None, None], diff, -jnp.inf))
        scores = jnp.einsum('bthk,bshk,btshk->btsh', qc, kc, decay)
        o_intra = jnp.einsum('btsh,bshv->bthv', scores, vc)
        b_last = b[:, -1]
        S_new = jnp.exp(b_last)[..., None] * S + jnp.einsum(
            'bshk,bshv->bhkv', kc * jnp.exp(b_last[:, None] - b), vc)
        return S_new, o_inter + o_intra

    S, o = lax.scan(step, S0.astype(jnp.float32),
                    (to_chunks(q), to_chunks(k), to_chunks(v), to_chunks(log_f)))
    o = jnp.moveaxis(o, 0, 1).reshape(B, T, H, DV)
    return o.astype(q.dtype), S.astype(S0.dtype)


def hgrn2_mixer(xn, w_in, lb, g_norm, w_out, S0):
    B, T, _ = xn.shape
    q, f, i, g = jnp.split(xn @ w_in, 4, axis=-1)
    q = jax.nn.silu(q).reshape(B, T, HGRN_HEADS, HGRN_DK)
    fg = lb + (1.0 - lb) * jax.nn.sigmoid(f.astype(jnp.float32))
    log_f = jnp.log(fg).reshape(B, T, HGRN_HEADS, HGRN_DK)
    k = (1.0 - fg).astype(xn.dtype).reshape(B, T, HGRN_HEADS, HGRN_DK)
    v = i.reshape(B, T, HGRN_HEADS, HGRN_DV)
    o, S = gla_chunked(q, k, v, log_f, S0)
    o = rmsnorm(o, g_norm.reshape(HGRN_HEADS, HGRN_DV))
    o = o.reshape(B, T, HGRN_WIDTH) * jax.nn.silu(g)
    return o @ w_out, S


def shared_kv(h, kv_norm, w_kv, pos):
    B, T, _ = h.shape
    k, v = jnp.split(rmsnorm(h, kv_norm) @ w_kv, 2, axis=-1)
    k = rope(k.reshape(B, T, ATTN_KV_HEADS, ATTN_HEAD_DIM), pos)
    v = v.reshape(B, T, ATTN_KV_HEADS, ATTN_HEAD_DIM)
    return k, v


def sink_attention(q, k, v, q_pos, k_pos, sinks):
    s = jnp.einsum('bnqhgd,bnkhd->bnhgqk', q, k).astype(jnp.float32) * (ATTN_HEAD_DIM ** -0.5)
    qp = q_pos[:, :, None]
    kp = k_pos[:, None, :]
    mask = (kp >= 0) & (kp <= qp) & (qp - kp < WINDOW)
    s = jnp.where(mask[None, :, None, None], s, -jnp.inf)
    sink = sinks.astype(jnp.float32).reshape(ATTN_KV_HEADS, ATTN_GROUP)[None, None, :, :, None, None]
    m = jnp.maximum(jnp.max(s, axis=-1, keepdims=True), sink)
    p = jnp.exp(s - m)
    p = p / (jnp.sum(p, axis=-1, keepdims=True) + jnp.exp(sink - m))
    return jnp.einsum('bnhgqk,bnkhd->bnqhgd', p.astype(v.dtype), v)


def swa_prompt(xn, w_q, sinks, w_out, k, v, pos):
    B, T, _ = xn.shape
    BLK = WINDOW
    nb = T // BLK
    q = rope((xn @ w_q).reshape(B, T, ATTN_Q_HEADS, ATTN_HEAD_DIM), pos)
    q = q.reshape(B, nb, BLK, ATTN_KV_HEADS, ATTN_GROUP, ATTN_HEAD_DIM)
    pad = ((0, 0), (BLK, 0), (0, 0), (0, 0))
    k_pad, v_pad = jnp.pad(k, pad), jnp.pad(v, pad)

    def band(a):
        return jnp.concatenate([a[:, :T].reshape(B, nb, BLK, *a.shape[2:]),
                                a[:, BLK:].reshape(B, nb, BLK, *a.shape[2:])], axis=2)

    kp_pad = jnp.arange(-BLK, T, dtype=jnp.int32)
    kp = jnp.concatenate([kp_pad[:T].reshape(nb, BLK), kp_pad[BLK:].reshape(nb, BLK)], axis=1)
    o = sink_attention(q, band(k_pad), band(v_pad), pos.reshape(nb, BLK), kp, sinks)
    return o.reshape(B, T, ATTN_Q_HEADS * ATTN_HEAD_DIM) @ w_out


def swa_sample(xn, w_q, sinks, w_out, k_all, v_all, q_pos, k_pos):
    B, T, _ = xn.shape
    q = rope((xn @ w_q).reshape(B, T, ATTN_Q_HEADS, ATTN_HEAD_DIM), q_pos)
    q = q.reshape(B, 1, T, ATTN_KV_HEADS, ATTN_GROUP, ATTN_HEAD_DIM)
    o = sink_attention(q, k_all[:, None], v_all[:, None], q_pos[None], k_pos[None], sinks)
    return o.reshape(B, T, ATTN_Q_HEADS * ATTN_HEAD_DIM) @ w_out


def trunk(x, pos, S_init, past_k, past_v, past_pos,
          norm_mix_pre, norm_mix_post, norm_ffn_pre, norm_ffn_post,
          hgrn_w_in, hgrn_lower_bounds, hgrn_g_norm, hgrn_w_out,
          kv_norm, w_kv, attn_w_q, attn_sinks, attn_w_out,
          ffn_w_gate_up, ffn_w_down):
    lbs = jnp.cumsum(jax.nn.softmax(hgrn_lower_bounds.astype(jnp.float32), axis=0), axis=0)
    h = x
    new_states = []
    k_sh = v_sh = None
    for l in range(DEPTH):
        hn = rmsnorm(h, norm_mix_pre[l])
        if l < N_A_LAYERS:
            mix, S_l = hgrn2_mixer(hn, hgrn_w_in[l], lbs[l], hgrn_g_norm[l], hgrn_w_out[l], S_init[l])
            new_states.append(S_l)
        else:
            j = l - N_A_LAYERS
            if past_k is None:
                mix = swa_prompt(hn, attn_w_q[j], attn_sinks[j], attn_w_out[j], k_sh, v_sh, pos)
            else:
                mix = swa_sample(hn, attn_w_q[j], attn_sinks[j], attn_w_out[j],
                                 jnp.concatenate([past_k, k_sh], axis=1),
                                 jnp.concatenate([past_v, v_sh], axis=1),
                                 pos, jnp.concatenate([past_pos, pos]))
        h = h + rmsnorm(mix, norm_mix_post[l])
        h = h + rmsnorm(swiglu(rmsnorm(h, norm_ffn_pre[l]), ffn_w_gate_up[l], ffn_w_down[l]),
                        norm_ffn_post[l])
        if l == N_A_LAYERS - 1:
            k_sh, v_sh = shared_kv(h, kv_norm, w_kv, pos)
    return h, jnp.stack(new_states), k_sh, v_sh


def setup_inputs(seed: int = 0) -> dict:
    key = jax.random.key(seed)
    ks = jax.random.split(key, 24)
    f32 = jnp.float32

    def w(k, shape, fan_in):
        return jax.random.normal(k, shape, f32) * (fan_in ** -0.5)

    def gain(k, shape):
        return 1.0 + 0.02 * jax.random.normal(k, shape, f32)

    w_buf = min(WINDOW, PAST_LEN)
    return {
        "x_prompt": jax.random.normal(ks[0], (BATCH, SEQ, D_MODEL), f32),
        "x_sample": jax.random.normal(ks[1], (DEC_BATCH, DEC_SEQ, D_MODEL), f32),
        "state_hgrn": jax.random.normal(ks[2], (N_A_LAYERS, DEC_BATCH, HGRN_HEADS, HGRN_DK, HGRN_DV), f32),
        "cache_k_win": jax.random.normal(ks[3], (DEC_BATCH, w_buf, ATTN_KV_HEADS, ATTN_HEAD_DIM), f32),
        "cache_v_win": jax.random.normal(ks[4], (DEC_BATCH, w_buf, ATTN_KV_HEADS, ATTN_HEAD_DIM), f32),
        "norm_mix_pre": gain(ks[5], (DEPTH, D_MODEL)),
        "norm_mix_post": gain(ks[6], (DEPTH, D_MODEL)),
        "norm_ffn_pre": gain(ks[7], (DEPTH, D_MODEL)),
        "norm_ffn_post": gain(ks[8], (DEPTH, D_MODEL)),
        "hgrn_w_in": w(ks[9], (N_A_LAYERS, D_MODEL, 4 * HGRN_WIDTH), D_MODEL),
        "hgrn_lower_bounds": jax.random.normal(ks[10], (N_A_LAYERS + 1, HGRN_WIDTH), f32),
        "hgrn_g_norm": gain(ks[11], (N_A_LAYERS, HGRN_HEADS * HGRN_DV)),
        "hgrn_w_out": w(ks[12], (N_A_LAYERS, HGRN_WIDTH, D_MODEL), HGRN_WIDTH),
        "kv_norm": gain(ks[13], (D_MODEL,)),
        "w_kv": w(ks[14], (D_MODEL, 2 * ATTN_KV_HEADS * ATTN_HEAD_DIM), D_MODEL),
        "attn_w_q": w(ks[15], (N_B_LAYERS, D_MODEL, ATTN_Q_HEADS * ATTN_HEAD_DIM), D_MODEL),
        "attn_sinks": 0.5 * jax.random.normal(ks[16], (N_B_LAYERS, ATTN_Q_HEADS), f32),
        "attn_w_out": w(ks[17], (N_B_LAYERS, ATTN_Q_HEADS * ATTN_HEAD_DIM, D_MODEL), ATTN_Q_HEADS * ATTN_HEAD_DIM),
        "ffn_w_gate_up": w(ks[18], (DEPTH, D_MODEL, 2 * D_FF), D_MODEL),
        "ffn_w_down": w(ks[19], (DEPTH, D_FF, D_MODEL), D_FF),
    }


def reference(x_prompt, x_sample, state_hgrn, cache_k_win, cache_v_win,
              norm_mix_pre, norm_mix_post, norm_ffn_pre, norm_ffn_post,
              hgrn_w_in, hgrn_lower_bounds, hgrn_g_norm, hgrn_w_out,
              kv_norm, w_kv, attn_w_q, attn_sinks, attn_w_out,
              ffn_w_gate_up, ffn_w_down):
    weights = (norm_mix_pre, norm_mix_post, norm_ffn_pre, norm_ffn_post,
               hgrn_w_in, hgrn_lower_bounds, hgrn_g_norm, hgrn_w_out,
               kv_norm, w_kv, attn_w_q, attn_sinks, attn_w_out,
               ffn_w_gate_up, ffn_w_down)
    Bp, Tp, _ = x_prompt.shape
    Ts = x_sample.shape[1]
    pos_p = jnp.arange(Tp, dtype=jnp.int32)
    S0_p = jnp.zeros((N_A_LAYERS, Bp, HGRN_HEADS, HGRN_DK, HGRN_DV), x_prompt.dtype)
    y_prompt, S_p, k_p, v_p = trunk(x_prompt, pos_p, S0_p, None, None, None, *weights)
    w_buf = cache_k_win.shape[1]
    pos_s = PAST_LEN + jnp.arange(Ts, dtype=jnp.int32)
    past_pos = PAST_LEN - w_buf + jnp.arange(w_buf, dtype=jnp.int32)
    y_sample, S_s, k_s, v_s = trunk(x_sample, pos_s, state_hgrn, cache_k_win, cache_v_win, past_pos, *weights)
    w_p = min(WINDOW, Tp)
    return (y_prompt, y_sample, S_p, S_s, k_p[:, Tp - w_p:], v_p[:, Tp - w_p:], k_s, v_s)
```

```python
import functools

import jax
import jax.numpy as jnp
from jax import lax
from jax.experimental import pallas as pl
from jax.experimental.pallas import tpu as pltpu

F32 = jnp.float32
BF16 = jnp.bfloat16

PAST_LEN = 8192
WINDOW = 128
ROPE_THETA = 500000.0
RMS_EPS = 1e-6
LANES = 128
GLA_CHUNK = 128
VMEM_LIMIT = 56 * 1024 * 1024

_NN = (((1,), (0,)), ((), ()))
_NT = (((1,), (1,)), ((), ()))
_TN = (((0,), (0,)), ((), ()))


def _dot(a, b, dims=_NN):
    return lax.dot_general(a, b, dims, preferred_element_type=F32)


def _rms(x, g):
    return x * lax.rsqrt(jnp.mean(x * x, axis=-1, keepdims=True) + RMS_EPS) * g


def _sigmoid(x):
    return 1.0 / (1.0 + jnp.exp(-x))


def _split3(x):
    hi = x.astype(BF16)
    r = x - hi.astype(F32)
    mid = r.astype(BF16)
    lo = (r - mid.astype(F32)).astype(BF16)
    return hi, mid, lo


def _dot_sel(sel, x):
    hi, mid, lo = _split3(x)
    return _dot(sel, hi) + _dot(sel, mid) + _dot(sel, lo)


def _params(sem):
    return pltpu.CompilerParams(dimension_semantics=sem, vmem_limit_bytes=VMEM_LIMIT)


def _inproj_kernel(h_ref, gpre_ref, wq_ref, wf_ref, wi_ref, wg_ref, lbp_ref,
                   q_ref, k_ref, v_ref, lf_ref, gs_ref, xn_ref, *, layer):
    @pl.when(pl.program_id(1) == 0)
    def _():
        xn_ref[...] = _rms(h_ref[...], gpre_ref[...]).astype(BF16)

    xn = xn_ref[...]
    q = _dot(xn, wq_ref[...])
    f = _dot(xn, wf_ref[...])
    lbp = lbp_ref[...]
    e = jnp.exp(lbp - jnp.max(lbp, axis=0, keepdims=True))
    lb = jnp.sum(e[:layer + 1], axis=0, keepdims=True) / jnp.sum(e, axis=0, keepdims=True)
    fg = lb + (1.0 - lb) * _sigmoid(f)
    q_ref[...] = q * _sigmoid(q)
    lf_ref[...] = jnp.log(fg)
    k_ref[...] = 1.0 - fg
    v_ref[...] = _dot(xn, wi_ref[...])
    g = _dot(xn, wg_ref[...])
    gs_ref[...] = g * _sigmoid(g)


def _inproj(h, gpre, w_in, lbp, *, layer, tm, tn):
    m, d = h.shape
    w = w_in.shape[1] // 4
    nj = w // tn
    wspecs = [pl.BlockSpec((d, tn), functools.partial(lambda i, j, s: (0, s * nj + j), s=s))
              for s in range(4)]
    ospec = pl.BlockSpec((tm, tn), lambda i, j: (i, j))
    return pl.pallas_call(
        functools.partial(_inproj_kernel, layer=layer),
        grid=(m // tm, nj),
        in_specs=[pl.BlockSpec((tm, d), lambda i, j: (i, 0)),
                  pl.BlockSpec((1, d), lambda i, j: (0, 0)),
                  *wspecs,
                  pl.BlockSpec((lbp.shape[0], tn), lambda i, j: (0, j))],
        out_specs=[ospec] * 5,
        out_shape=[jax.ShapeDtypeStruct((m, w), F32)] * 5,
        scratch_shapes=[pltpu.VMEM((tm, d), BF16)],
        compiler_params=_params(("parallel", "arbitrary")),
        name="hgrn_inproj",
    )(h, gpre, w_in, w_in, w_in, w_in, lbp)


def _head_out(o, gn, gs):
    return (_rms(o, gn) * gs).astype(BF16)


def _gla_prompt_kernel(q_ref, k_ref, v_ref, lf_ref, gs_ref, gn_ref, og_ref, s_ref,
                       st_ref, b_ref, *, t_len, heads):
    c = GLA_CHUNK
    row = lax.broadcasted_iota(jnp.int32, (c, c), 0)
    col = lax.broadcasted_iota(jnp.int32, (c, c), 1)
    tri = jnp.where(row >= col, 1.0, 0.0).astype(BF16)
    eye = row == col
    sub = lax.broadcasted_iota(jnp.int32, (8, LANES), 0)
    levels = []
    w = c // 2
    while w >= 1:
        second = (row % (2 * w)) >= w
        mask = (row // (2 * w) == col // (2 * w)) & second & ((col % (2 * w)) < w)
        levels.append((w, second, mask))
        w //= 2
    st_ref[...] = jnp.zeros_like(st_ref)

    def ref_rows(hh, w):
        pieces = []
        for g in range(c // 8):
            if 2 * w >= 8:
                rr = (8 * g) // (2 * w) * (2 * w) + w - 1
                pieces.append(jnp.broadcast_to(b_ref[hh, pl.ds(rr, 1), :], (8, LANES)))
            else:
                acc = None
                for u in range(8 // (2 * w)):
                    rr = 8 * g + 2 * w * u + w - 1
                    bc = jnp.broadcast_to(b_ref[hh, pl.ds(rr, 1), :], (8, LANES))
                    acc = bc if acc is None else jnp.where(sub >= 2 * w * u, bc, acc)
                pieces.append(acc)
        return jnp.concatenate(pieces, axis=0)

    def chunk(ci, carry):
        r0 = pl.multiple_of(ci * c, c)
        for hh in range(heads):
            ls = pl.ds(hh * LANES, LANES)
            q = q_ref[pl.ds(r0, c), ls]
            k = k_ref[pl.ds(r0, c), ls]
            v = v_ref[pl.ds(r0, c), ls]
            vb = v.astype(BF16)
            b = _dot_sel(tri, lf_ref[pl.ds(r0, c), ls])
            b_ref[hh] = b
            blast = b[c - 1:c, :]
            st = st_ref[hh]
            o = _dot((q * jnp.exp(b)).astype(BF16), st.astype(BF16), _NT)
            kd = (k * jnp.exp(blast - b)).astype(BF16)
            st_ref[hh] = jnp.exp(blast) * st + _dot(vb, kd, _TN)
            a = jnp.where(eye, jnp.sum(q * k, axis=-1, keepdims=True), 0.0)
            for w, second, mask in levels:
                d = b - ref_rows(hh, w)
                e = jnp.exp(jnp.where(second, d, -d))
                al = _dot((q * e).astype(BF16), (k * e).astype(BF16), _NT)
                a = jnp.where(mask, al, a)
            o = o + _dot(a.astype(BF16), vb)
            og_ref[pl.ds(r0, c), ls] = _head_out(o, gn_ref[:, ls], gs_ref[pl.ds(r0, c), ls])
        return carry

    lax.fori_loop(0, t_len // c, chunk, 0)
    for hh in range(heads):
        s_ref[0, 0, hh] = st_ref[hh].T


def _gla_prompt(q, k, v, lf, gs, gn, *, batch, t_len, heads_per_step=2):
    w = q.shape[1]
    nh = w // LANES
    hb = heads_per_step
    spec = pl.BlockSpec((t_len, hb * LANES), lambda b, j: (b, j))
    return pl.pallas_call(
        functools.partial(_gla_prompt_kernel, t_len=t_len, heads=hb),
        grid=(batch, nh // hb),
        in_specs=[spec] * 5 + [pl.BlockSpec((1, hb * LANES), lambda b, j: (0, j))],
        out_specs=[spec, pl.BlockSpec((1, 1, hb, LANES, LANES), lambda b, j: (0, b, j, 0, 0))],
        out_shape=[jax.ShapeDtypeStruct((batch * t_len, w), BF16),
                   jax.ShapeDtypeStruct((1, batch, nh, LANES, LANES), F32)],
        scratch_shapes=[pltpu.VMEM((hb, LANES, LANES), F32),
                        pltpu.VMEM((hb, GLA_CHUNK, LANES), F32)],
        compiler_params=_params(("parallel", "parallel")),
        name="hgrn_mixer_prompt",
    )(q, k, v, lf, gs, gn)


def _gla_sample_kernel(q_ref, k_ref, v_ref, lf_ref, gs_ref, gn_ref, s0_ref, og_ref, s_ref, *, t_len):
    r = q_ref.shape[0]
    nb = r // t_len
    row = lax.broadcasted_iota(jnp.int32, (r, r), 0)
    col = lax.broadcasted_iota(jnp.int32, (r, r), 1)
    same = row // t_len == col // t_len
    cum = jnp.where(same & (row >= col), 1.0, 0.0).astype(BF16)
    tot = jnp.where(same, 1.0, 0.0).astype(BF16)
    rid = lax.broadcasted_iota(jnp.int32, (r, LANES), 0)
    rid16 = lax.broadcasted_iota(jnp.int32, (16, LANES), 0)

    q = q_ref[...]
    k = k_ref[...]
    v = v_ref[...]
    lf = lf_ref[...]
    b = _dot_sel(cum, lf)
    blast = _dot_sel(tot, lf)
    qe = q * jnp.exp(b)
    kd = k * jnp.exp(blast - b)
    e = jnp.exp(blast)

    o = jnp.sum(q * k, axis=-1, keepdims=True) * v
    for d in range(1, t_len):
        valid = (rid % t_len) >= d
        ks = pltpu.roll(k, d, axis=0)
        bs = pltpu.roll(b, d, axis=0)
        vs = pltpu.roll(v, d, axis=0)
        p = q * ks * jnp.exp(jnp.where(valid, b - bs, -jnp.inf))
        o = o + jnp.sum(p, axis=-1, keepdims=True) * vs

    ones = jnp.ones((16, LANES), BF16)
    zeros = jnp.zeros((16, LANES), BF16)
    per_tile = 16 // t_len
    o_tiles = []
    for ti in range(r // 16):
        rows = slice(16 * ti, 16 * ti + 16)
        qe16, kd16, e16, v16 = qe[rows], kd[rows], e[rows], v[rows].astype(BF16)
        rhs = jnp.concatenate([jnp.concatenate([v16, zeros], axis=1)]
                              + [jnp.concatenate([zeros, ones], axis=1)] * 3, axis=0)
        o16 = o[rows]
        for u in range(per_tile):
            bl = ti * per_tile + u
            grp = (rid16 // t_len) == u
            first = rid16 == u * t_len
            s_old = s0_ref[0, bl, 0]
            o16 = o16 + _dot(jnp.where(grp, qe16, 0.0).astype(BF16), s_old.astype(BF16))
            lhs = jnp.concatenate([jnp.where(grp, kd16, 0.0).astype(BF16),
                                   *_split3(jnp.where(first, e16, 0.0))], axis=0)
            ud = _dot(lhs, rhs, _TN)
            s_ref[0, bl, 0] = ud[:, LANES:] * s_old + ud[:, :LANES]
        o_tiles.append(o16)
    o = jnp.concatenate(o_tiles, axis=0)
    og_ref[...] = _head_out(o, gn_ref[...], gs_ref[...])


def _gla_sample(q, k, v, lf, gs, gn, s0, *, row0, batch, t_len, rows_per_step=128):
    w = q.shape[1]
    nh = w // LANES
    r = rows_per_step
    nb = r // t_len
    blk0 = row0 // r
    spec = pl.BlockSpec((r, LANES), lambda i, h: (blk0 + i, h))
    sspec = pl.BlockSpec((1, nb, 1, LANES, LANES), lambda i, h: (0, i, h, 0, 0))
    return pl.pallas_call(
        functools.partial(_gla_sample_kernel, t_len=t_len),
        grid=(batch // nb, nh),
        in_specs=[spec] * 5 + [pl.BlockSpec((1, LANES), lambda i, h: (0, h)), sspec],
        out_specs=[pl.BlockSpec((r, LANES), lambda i, h: (i, h)), sspec],
        out_shape=[jax.ShapeDtypeStruct((batch * t_len, w), BF16),
                   jax.ShapeDtypeStruct(s0.shape, F32)],
        compiler_params=_params(("parallel", "parallel")),
        name="hgrn_mixer_sample",
    )(q, k, v, lf, gs, gn, s0)


def _outproj_kernel(og_ref, h_ref, w_ref, gpost_ref, out_ref):
    mix = _dot(og_ref[...], w_ref[...])
    out_ref[...] = h_ref[...] + _rms(mix, gpost_ref[...])


def _outproj(og, h, w, gpost, *, tm):
    m, d = h.shape
    kdim = og.shape[1]
    return pl.pallas_call(
        _outproj_kernel,
        grid=(m // tm,),
        in_specs=[pl.BlockSpec((tm, kdim), lambda i: (i, 0)),
                  pl.BlockSpec((tm, d), lambda i: (i, 0)),
                  pl.BlockSpec((kdim, d), lambda i: (0, 0)),
                  pl.BlockSpec((1, d), lambda i: (0, 0))],
        out_specs=pl.BlockSpec((tm, d), lambda i: (i, 0)),
        out_shape=jax.ShapeDtypeStruct((m, d), F32),
        compiler_params=_params(("parallel",)),
        name="mixer_outproj",
    )(og, h, w, gpost)


def _ffn_kernel(h_ref, gpre_ref, gpost_ref, wg_ref, wu_ref, wd_ref, out_ref, xn_ref):
    j = pl.program_id(1)

    @pl.when(j == 0)
    def _():
        xn_ref[...] = _rms(h_ref[...], gpre_ref[...]).astype(BF16)
        out_ref[...] = jnp.zeros_like(out_ref)

    xn = xn_ref[...]
    gate = _dot(xn, wg_ref[...])
    up = _dot(xn, wu_ref[...])
    act = (gate * _sigmoid(gate) * up).astype(BF16)
    out_ref[...] += _dot(act, wd_ref[...])

    @pl.when(j == pl.num_programs(1) - 1)
    def _():
        out_ref[...] = h_ref[...] + _rms(out_ref[...], gpost_ref[...])


def _ffn(h, gpre, gpost, w_gate_up, w_down, *, tm, tf):
    m, d = h.shape
    dff = w_down.shape[0]
    assert dff % tf == 0 and m % tm == 0
    nj = dff // tf
    return pl.pallas_call(
        _ffn_kernel,
        grid=(m // tm, nj),
        in_specs=[pl.BlockSpec((tm, d), lambda i, j: (i, 0)),
                  pl.BlockSpec((1, d), lambda i, j: (0, 0)),
                  pl.BlockSpec((1, d), lambda i, j: (0, 0)),
                  pl.BlockSpec((d, tf), lambda i, j: (0, j)),
                  pl.BlockSpec((d, tf), lambda i, j: (0, nj + j)),
                  pl.BlockSpec((tf, d), lambda i, j: (j, 0))],
        out_specs=pl.BlockSpec((tm, d), lambda i, j: (i, 0)),
        out_shape=jax.ShapeDtypeStruct((m, d), F32),
        scratch_shapes=[pltpu.VMEM((tm, d), BF16)],
        compiler_params=_params(("parallel", "arbitrary")),
        name="swiglu_ffn",
    )(h, gpre, gpost, w_gate_up, w_gate_up, w_down)


def _rope(x, cos, sa, sb):
    return x * cos + pltpu.roll(x, LANES - 8, axis=1) * sa + pltpu.roll(x, 8, axis=1) * sb


def _qkv_kernel(h_ref, gq_ref, gkv_ref, wq_ref, wkv_ref, cos_ref, sa_ref, sb_ref, q_ref, kv_ref):
    x = h_ref[...]
    inv = lax.rsqrt(jnp.mean(x * x, axis=-1, keepdims=True) + RMS_EPS)
    cos, sa, sb = cos_ref[...], sa_ref[...], sb_ref[...]
    kv = _dot((x * inv * gkv_ref[...]).astype(BF16), wkv_ref[...])
    nk = kv.shape[1] // 2
    for s in range(nk // LANES):
        ls = slice(s * LANES, (s + 1) * LANES)
        kv_ref[:, ls] = _rope(kv[:, ls], cos, sa, sb)
    kv_ref[:, nk:] = kv[:, nk:]
    q = _dot((x * inv * gq_ref[...]).astype(BF16), wq_ref[...])
    for s in range(q.shape[1] // LANES):
        ls = slice(s * LANES, (s + 1) * LANES)
        q_ref[:, ls] = _rope(q[:, ls], cos, sa, sb).astype(BF16)


def _qkv(h, gq, gkv, wq, wkv, cos, sa, sb, *, tm):
    m, d = h.shape
    nq, nkv = wq.shape[1], wkv.shape[1]
    tab = pl.BlockSpec((tm, LANES), lambda i: (i, 0))
    vec = pl.BlockSpec((1, d), lambda i: (0, 0))
    return pl.pallas_call(
        _qkv_kernel,
        grid=(m // tm,),
        in_specs=[pl.BlockSpec((tm, d), lambda i: (i, 0)), vec, vec,
                  pl.BlockSpec((d, nq), lambda i: (0, 0)),
                  pl.BlockSpec((d, nkv), lambda i: (0, 0)),
                  tab, tab, tab],
        out_specs=[pl.BlockSpec((tm, nq), lambda i: (i, 0)),
                   pl.BlockSpec((tm, nkv), lambda i: (i, 0))],
        out_shape=[jax.ShapeDtypeStruct((m, nq), BF16),
                   jax.ShapeDtypeStruct((m, nkv), F32)],
        compiler_params=_params(("parallel",)),
        name="attn_qkv_proj",
    )(h, gq, gkv, wq, wkv, cos, sa, sb)


def _head_halves(k, left):
    return [jnp.where(left, k, 0.0).astype(BF16), jnp.where(left, 0.0, k).astype(BF16)]


def _sink_softmax_pv(parts, sink):
    ms = [jnp.where(mask, s, -jnp.inf) for s, mask, _ in parts]
    m = sink
    for s in ms:
        m = jnp.maximum(m, jnp.max(s, axis=-1, keepdims=True))
    ps = [jnp.exp(s - m) for s in ms]
    den = jnp.exp(sink - m)
    for p in ps:
        den = den + jnp.sum(p, axis=-1, keepdims=True)
    out = None
    for p, (_, _, vb) in zip(ps, parts):
        t = _dot((p / den).astype(BF16), vb)
        out = t if out is None else out + t
    return out


def _attn_prompt_kernel(sink_ref, q_ref, kvp_ref, kvc_ref, o_ref, *, head_dim):
    i = pl.program_id(1)
    nq = q_ref.shape[1]
    nk = kvc_ref.shape[1] // 2
    blk = q_ref.shape[0]
    scale = head_dim ** -0.5
    r = lax.broadcasted_iota(jnp.int32, (blk, blk), 0)
    c = lax.broadcasted_iota(jnp.int32, (blk, blk), 1)
    mask_cur = c <= r
    mask_prev = (c > r) & (i > 0)
    left = lax.broadcasted_iota(jnp.int32, (blk, LANES), 1) < head_dim
    slabs_per_kv = (nq // LANES) // (nk // LANES)
    for ks in range(nk // LANES):
        kc = _head_halves(kvc_ref[:, ks * LANES:(ks + 1) * LANES], left)
        kp = _head_halves(kvp_ref[:, ks * LANES:(ks + 1) * LANES], left)
        vc = kvc_ref[:, nk + ks * LANES:nk + (ks + 1) * LANES].astype(BF16)
        vp = kvp_ref[:, nk + ks * LANES:nk + (ks + 1) * LANES].astype(BF16)
        for s in range(ks * slabs_per_kv, (ks + 1) * slabs_per_kv):
            qs = q_ref[:, s * LANES:(s + 1) * LANES]
            outs = []
            for side in range(2):
                sc = _dot(qs, kc[side], _NT) * scale
                sp = _dot(qs, kp[side], _NT) * scale
                outs.append(_sink_softmax_pv([(sc, mask_cur, vc), (sp, mask_prev, vp)],
                                             sink_ref[2 * s + side]))
            o_ref[:, s * LANES:(s + 1) * LANES] = jnp.where(left, outs[0], outs[1]).astype(BF16)


def _attn_prompt(sinks, q, kv, *, batch, t_len, head_dim):
    nq = q.shape[1]
    nkv = kv.shape[1]
    blk = WINDOW
    nb = t_len // blk
    return pl.pallas_call(
        functools.partial(_attn_prompt_kernel, head_dim=head_dim),
        grid=(batch, nb),
        in_specs=[pl.BlockSpec(memory_space=pltpu.SMEM),
                  pl.BlockSpec((blk, nq), lambda b, i: (b * nb + i, 0)),
                  pl.BlockSpec((blk, nkv), lambda b, i: (b * nb + jnp.maximum(i - 1, 0), 0)),
                  pl.BlockSpec((blk, nkv), lambda b, i: (b * nb + i, 0))],
        out_specs=pl.BlockSpec((blk, nq), lambda b, i: (b * nb + i, 0)),
        out_shape=jax.ShapeDtypeStruct((batch * t_len, nq), BF16),
        compiler_params=_params(("parallel", "parallel")),
        name="swa_prompt",
    )(sinks, q, kv, kv)


def _attn_sample_kernel(sink_ref, q_ref, kn_ref, vn_ref, ck_ref, cv_ref, o_ref, knb_ref, vnb_ref,
                        *, head_dim, t_len):
    nq = q_ref.shape[2]
    nk = ck_ref.shape[2]
    rows = q_ref.shape[1]
    wlen = ck_ref.shape[1]
    scale = head_dim ** -0.5
    t = lax.broadcasted_iota(jnp.int32, (rows, wlen), 0)
    c = lax.broadcasted_iota(jnp.int32, (rows, wlen), 1)
    mask_cache = c >= t + 1 + (wlen - WINDOW)
    mask_new = (c <= t) & (c < t_len)
    left = lax.broadcasted_iota(jnp.int32, (rows, LANES), 1) < head_dim
    left_k = lax.broadcasted_iota(jnp.int32, (wlen, LANES), 1) < head_dim
    slabs_per_kv = (nq // LANES) // (nk // LANES)
    knb_ref[...] = jnp.zeros_like(knb_ref)
    vnb_ref[...] = jnp.zeros_like(vnb_ref)

    def per_batch(bi, carry):
        knb_ref[0:kn_ref.shape[1], :] = kn_ref[bi]
        vnb_ref[0:vn_ref.shape[1], :] = vn_ref[bi]
        for ks in range(nk // LANES):
            ls = slice(ks * LANES, (ks + 1) * LANES)
            kc = _head_halves(ck_ref[bi, :, ls], left_k)
            kn = _head_halves(knb_ref[:, ls], left_k)
            vc = cv_ref[bi, :, ls].astype(BF16)
            vn = vnb_ref[:, ls].astype(BF16)
            for s in range(ks * slabs_per_kv, (ks + 1) * slabs_per_kv):
                qs = q_ref[bi, :, s * LANES:(s + 1) * LANES]
                outs = []
                for side in range(2):
                    sc = _dot(qs, kc[side], _NT) * scale
                    sn = _dot(qs, kn[side], _NT) * scale
                    outs.append(_sink_softmax_pv([(sc, mask_cache, vc), (sn, mask_new, vn)],
                                                 sink_ref[2 * s + side]))
                o_ref[bi, :, s * LANES:(s + 1) * LANES] = (
                    jnp.where(left, outs[0], outs[1]).astype(BF16))
        return carry

    lax.fori_loop(0, q_ref.shape[0], per_batch, 0)


def _attn_sample(sinks, q, kn, vn, ck, cv, *, head_dim, t_len, bb=8):
    batch, rows, nq = q.shape
    wlen, nk = ck.shape[1], ck.shape[2]
    b3 = lambda n, m_: pl.BlockSpec((bb, n, m_), lambda i: (i, 0, 0))
    return pl.pallas_call(
        functools.partial(_attn_sample_kernel, head_dim=head_dim, t_len=t_len),
        grid=(batch // bb,),
        in_specs=[pl.BlockSpec(memory_space=pltpu.SMEM),
                  b3(rows, nq), b3(kn.shape[1], nk), b3(vn.shape[1], nk), b3(wlen, nk), b3(wlen, nk)],
        out_specs=b3(rows, nq),
        out_shape=jax.ShapeDtypeStruct((batch, rows, nq), BF16),
        scratch_shapes=[pltpu.VMEM((wlen, nk), F32), pltpu.VMEM((wlen, nk), F32)],
        compiler_params=_params(("parallel",)),
        name="swa_sample",
    )(sinks, q, kn, vn, ck, cv)


def _rope_tables(pos, head_dim):
    rot = head_dim // 4
    half = rot // 2
    inv = ROPE_THETA ** (-jnp.arange(0, rot, 2, dtype=F32) / rot)
    ang = pos.astype(F32)[:, None] * inv[None, :]
    cos, sin = jnp.cos(ang), jnp.sin(ang)
    n = pos.shape[0]
    pad = jnp.zeros((n, head_dim - rot), F32)
    zero = jnp.zeros((n, half), F32)
    c64 = jnp.concatenate([cos, cos, pad + 1.0], axis=1)
    a64 = jnp.concatenate([-sin, zero, pad], axis=1)
    b64 = jnp.concatenate([zero, sin, pad], axis=1)
    rep = LANES // head_dim
    return tuple(jnp.tile(x, (1, rep)) for x in (c64, a64, b64))


def kernel(x_prompt, x_sample, state_hgrn, cache_k_win, cache_v_win, norm_mix_pre, norm_mix_post,
           norm_ffn_pre, norm_ffn_post, hgrn_w_in, hgrn_lower_bounds, hgrn_g_norm, hgrn_w_out,
           kv_norm, w_kv, attn_w_q, attn_sinks, attn_w_out, ffn_w_gate_up, ffn_w_down):
    bp, tp, d = x_prompt.shape
    bs, ts, _ = x_sample.shape
    mp, ms = bp * tp, bs * ts
    kvh, hd = cache_k_win.shape[2], cache_k_win.shape[3]
    wbuf = cache_k_win.shape[1]
    qh = attn_sinks.shape[1]
    grp = qh // kvh
    nkd = kvh * hd
    tm = 512
    dff = ffn_w_down.shape[1]
    tf = 512 if dff % 512 == 0 else 256

    order = []
    for s in range(qh // 2):
        pair = s // grp
        for side in range(2):
            order.append((2 * pair + side) * grp + s % grp)
    order = jnp.asarray(order, jnp.int32)
    lanes_q = (order[:, None] * hd + jnp.arange(hd, dtype=jnp.int32)[None, :]).reshape(-1)

    bf = lambda a: a.astype(BF16)
    w_in = bf(hgrn_w_in[0])
    w_out = bf(hgrn_w_out[0])
    wq = bf(attn_w_q[0][:, lanes_q])
    wo = bf(attn_w_out[0][lanes_q, :])
    wkv = bf(w_kv)
    wgu = [bf(ffn_w_gate_up[l]) for l in range(2)]
    wdn = [bf(ffn_w_down[l]) for l in range(2)]
    sinks = attn_sinks[0][order].astype(F32)
    row = lambda a: a.reshape(1, -1)

    h = jnp.concatenate([x_prompt.reshape(mp, d), x_sample.reshape(ms, d)], axis=0)

    q, k, v, lf, gs = _inproj(h, row(norm_mix_pre[0]), w_in, hgrn_lower_bounds, layer=0, tm=tm, tn=512)
    gn = row(hgrn_g_norm[0])
    og_p, s_p = _gla_prompt(q, k, v, lf, gs, gn, batch=bp, t_len=tp)
    og_s, s_s = _gla_sample(q, k, v, lf, gs, gn, state_hgrn, row0=mp, batch=bs, t_len=ts)
    og = jnp.concatenate([og_p, og_s], axis=0)
    h = _outproj(og, h, w_out, row(norm_mix_post[0]), tm=tm)
    h = _ffn(h, row(norm_ffn_pre[0]), row(norm_ffn_post[0]), wgu[0], wdn[0], tm=tm, tf=tf)

    pos = jnp.concatenate([jnp.tile(jnp.arange(tp, dtype=jnp.int32), bp),
                           jnp.tile(PAST_LEN + jnp.arange(ts, dtype=jnp.int32), bs)])
    cos, sa, sb = _rope_tables(pos, hd)
    qa, kv = _qkv(h, row(norm_mix_pre[1]), row(kv_norm), wq, wkv, cos, sa, sb, tm=tm)
    oa_p = _attn_prompt(sinks, qa, kv, batch=bp, t_len=tp, head_dim=hd)
    rows = 16
    pad3 = lambda a, n: jnp.pad(a.reshape(bs, ts, -1), ((0, 0), (0, n - ts), (0, 0)))
    k_s, v_s = kv[mp:, :nkd], kv[mp:, nkd:]
    oa_s = _attn_sample(sinks, pad3(qa[mp:], rows), pad3(k_s, 8), pad3(v_s, 8),
                        cache_k_win.reshape(bs, wbuf, nkd), cache_v_win.reshape(bs, wbuf, nkd),
                        head_dim=hd, t_len=ts)
    oa = jnp.concatenate([oa_p, oa_s[:, :ts].reshape(ms, -1)], axis=0)
    h = _outproj(oa, h, wo, row(norm_mix_post[1]), tm=tm)
    h = _ffn(h, row(norm_ffn_pre[1]), row(norm_ffn_post[1]), wgu[1], wdn[1], tm=tm, tf=tf)

    wp = min(WINDOW, tp)
    kv_p = kv[:mp].reshape(bp, tp, 2 * nkd)[:, tp - wp:]
    return (h[:mp].reshape(bp, tp, d), h[mp:].reshape(bs, ts, d), s_p, s_s,
            kv_p[..., :nkd].reshape(bp, wp, kvh, hd), kv_p[..., nkd:].reshape(bp, wp, kvh, hd),
            k_s.reshape(bs, ts, kvh, hd), v_s.reshape(bs, ts, kvh, hd))
```
